```python
import math
import jax, jax.numpy as jnp
from jax import lax
import numpy as np

D_MODEL = 2048
BATCH = 4
SEQ = 2048
DEPTH = 2
DEC_BATCH = 8
DEC_SEQ = 4
PAST_LEN = 16384
PAGE_SIZE = 128

HD_A = 64
C_A = D_MODEL // 2
H_A = C_A // HD_A
LORA_W = C_A // 16
LORA_A = C_A // 16
LORA_G = C_A // 8
C_SHIFT = 3 * C_A + LORA_W + LORA_A + LORA_G
GN_EPS = 64e-5
HD_B = 128
C_B = D_MODEL // 2
H_B = C_B // HD_B
MOBA_BLOCK = 256
MOBA_TOPK = 3
Q_CHUNK = 16
N_BUCKETS = 32
MAX_DIST = 128
C_IN = C_SHIFT + 3 * C_B + 2 * D_MODEL
D_FF = 5632
CONV_W = 3
EPS = 1e-6

kernel_name = 'hybrid_rwkv7_moba_decoder_step'


def rmsnorm(x, g):
    xf = x.astype(jnp.float32)
    y = xf * lax.rsqrt(jnp.mean(xf * xf, axis=-1, keepdims=True) + EPS)
    return (y * g.astype(jnp.float32)).astype(x.dtype)


def rwkv7_mix(u, prev, S0, mu, w_decay_up, w_a_up, w_g_up, decay_base, a_base, k_k, k_a, r_k, gn_w, gn_b):
    B, T, _ = u.shape
    f32 = jnp.float32
    shifted = jnp.concatenate([prev[:, None, :].astype(u.dtype), u[:, :-1]], axis=1)
    xs = (u + (shifted - u) * mu).astype(f32)
    o1, o2, o3 = C_A, 2 * C_A, 3 * C_A
    o4 = o3 + LORA_W
    o5 = o4 + LORA_A
    r, k, v = xs[..., :o1], xs[..., o1:o2], xs[..., o2:o3]
    wd, ad, gd = xs[..., o3:o4], xs[..., o4:o5], xs[..., o5:]
    w_log = -jax.nn.softplus(-(decay_base + jnp.tanh(wd) @ w_decay_up)) - 0.5
    decay = jnp.exp(-jnp.exp(w_log))
    a = jax.nn.sigmoid(a_base + ad @ w_a_up)
    g = jax.nn.sigmoid(gd) @ w_g_up
    kk = (k * k_k).reshape(B, T, H_A, HD_A)
    kk = kk * lax.rsqrt(jnp.maximum(jnp.sum(kk * kk, axis=-1, keepdims=True), 1e-24))
    k = k * (1.0 + (a - 1.0) * k_a)
    hs = lambda z: z.reshape(B, T, H_A, HD_A)
    r, k, v, a, decay = hs(r), hs(k), hs(v), hs(a), hs(decay)

    def step(S, inp):
        r_t, w_t, k_t, v_t, kk_t, a_t = inp
        sk = jnp.einsum('bhij,bhj->bhi', S, kk_t)
        S = S * w_t[:, :, None, :] - sk[..., None] * (kk_t * a_t)[:, :, None, :] + v_t[..., None] * k_t[:, :, None, :]
        return S, jnp.einsum('bhij,bhj->bhi', S, r_t)

    tm = lambda z: jnp.moveaxis(z, 1, 0)
    S_fin, o = lax.scan(step, S0.astype(f32), (tm(r), tm(decay), tm(k), tm(v), tm(kk), tm(a)))
    o = jnp.moveaxis(o, 0, 1)
    mean = jnp.mean(o, axis=-1, keepdims=True)
    var = jnp.mean((o - mean) ** 2, axis=-1, keepdims=True)
    on = ((o - mean) * lax.rsqrt(var + GN_EPS)).reshape(B, T, C_A) * gn_w + gn_b
    bonus = (jnp.sum(r * k * r_k, axis=-1, keepdims=True) * v).reshape(B, T, C_A)
    y = (on + bonus) * g
    return y.astype(u.dtype), S_fin.astype(S0.dtype), u[:, -1]


def t5_bucket(rel):
    n = jnp.maximum(rel, 0)
    max_exact = N_BUCKETS // 2
    nf = jnp.maximum(n, 1).astype(jnp.float32)
    large = max_exact + (jnp.log(nf / max_exact) / math.log(MAX_DIST / max_exact) * (N_BUCKETS - max_exact)).astype(jnp.int32)
    large = jnp.minimum(large, N_BUCKETS - 1)
    return jnp.where(n < max_exact, n, large)


def moba_attend(q, k_all, v_all, q_pos, rel_bias):
    B, Q = q.shape[0], q.shape[1]
    L = k_all.shape[1]
    nb = -(-L // MOBA_BLOCK)
    pad = nb * MOBA_BLOCK - L

    def blocks(z):
        z = jnp.pad(z, ((0, 0), (0, pad), (0, 0), (0, 0)))
        return z.reshape(B, nb, MOBA_BLOCK, H_B, HD_B).transpose(0, 3, 1, 2, 4)

    kb, vb = blocks(k_all), blocks(v_all)
    kmean = jnp.mean(kb, axis=3, dtype=jnp.float32)
    topk = min(MOBA_TOPK, nb)
    qc = math.gcd(Q, Q_CHUNK)
    nc = Q // qc
    bi = jnp.arange(B)[:, None, None, None]
    hi = jnp.arange(H_B)[None, :, None, None]
    scale = HD_B ** -0.5
    f32 = jnp.float32

    def chunk(args):
        qx, px = args
        qh = qx.transpose(0, 2, 1, 3)
        own = px // MOBA_BLOCK
        gate = jnp.einsum('bhqd,bhnd->bhqn', qh, kmean, preferred_element_type=f32)
        past = jnp.arange(nb)[None, :] < own[:, None]
        gate = jnp.where(past, gate, -jnp.inf)
        top_s, top_i = lax.top_k(gate, topk)
        own_b = jnp.broadcast_to(own[None, None, :, None], (B, H_B, qc, 1))
        idx = jnp.concatenate([top_i, own_b], axis=-1)
        valid = jnp.concatenate([jnp.isfinite(top_s), jnp.ones((B, H_B, qc, 1), bool)], axis=-1)
        kg = kb[bi, hi, idx]
        vg = vb[bi, hi, idx]
        kpos = idx[..., None] * MOBA_BLOCK + jnp.arange(MOBA_BLOCK)
        rel = px[:, None, None] - kpos
        mask = valid[..., None] & (rel >= 0)
        bias = rel_bias[hi[..., None], t5_bucket(rel)].astype(f32)
        s = jnp.einsum('bhqd,bhqnkd->bhqnk', qh, kg, preferred_element_type=f32) * scale + bias
        s = jnp.where(mask, s, -jnp.inf)
        p = jax.nn.softmax(s.reshape(B, H_B, qc, -1), axis=-1).reshape(s.shape)
        o = jnp.einsum('bhqnk,bhqnkd->bhqd', p.astype(vg.dtype), vg, preferred_element_type=f32)
        return o.transpose(0, 2, 1, 3).astype(q.dtype)

    qs = q.reshape(B, nc, qc, H_B, HD_B).transpose(1, 0, 2, 3, 4)
    ps = q_pos.reshape(nc, qc)
    o = lax.map(chunk, (qs, ps))
    return o.transpose(1, 0, 2, 3, 4).reshape(B, Q, C_B)


def conv_ffn(h, buf, w_up, conv_w, conv_b, w_down):
    T = h.shape[1]
    ug = h @ w_up
    u, gt = ug[..., :D_FF], ug[..., D_FF:]
    ext = jnp.concatenate([buf.astype(u.dtype), u], axis=1)
    c = conv_b + sum(conv_w[j] * ext[:, j:j + T] for j in range(CONV_W))
    y = (jax.nn.gelu(c, approximate=True) * gt) @ w_down
    return y, ext[:, T:]


def layer_forward(x, q_pos, S0, shift0, conv0, past_k, past_v, rel_bias, lp):
    B, T, _ = x.shape
    h = rmsnorm(x, lp['ln_attn_pre'])
    proj = h @ lp['w_in']
    o_b = C_SHIFT + 3 * C_B
    y_a, S_new, shift_new = rwkv7_mix(proj[..., :C_SHIFT], shift0, S0, lp['mu_shift'], lp['w_decay_up'], lp['w_a_up'],
                                      lp['w_g_up'], lp['decay_base'], lp['a_base'], lp['k_k'], lp['k_a'], lp['r_k'],
                                      lp['gn_w'], lp['gn_b'])
    qkv = proj[..., C_SHIFT:o_b].reshape(B, T, 3, H_B, HD_B)
    q, k, v = qkv[:, :, 0], qkv[:, :, 1], qkv[:, :, 2]
    k_all = k if past_k is None else jnp.concatenate([past_k.astype(k.dtype), k], axis=1)
    v_all = v if past_v is None else jnp.concatenate([past_v.astype(v.dtype), v], axis=1)
    y_b = moba_attend(q, k_all, v_all, q_pos, rel_bias)
    g_a = jax.nn.sigmoid(proj[..., o_b:o_b + D_MODEL])
    g_b = jax.nn.sigmoid(proj[..., o_b + D_MODEL:])
    mix = g_a * (y_a @ lp['w_branch_a']) + g_b * (y_b @ lp['w_branch_b'])
    x = x + rmsnorm(mix @ lp['w_out'], lp['ln_attn_post'])
    f, conv_new = conv_ffn(rmsnorm(x, lp['ln_ffn_pre']), conv0, lp['w_ffn_up'], lp['ffn_conv_w'], lp['ffn_conv_b'], lp['w_ffn_down'])
    x = x + rmsnorm(f, lp['ln_ffn_post'])
    return x, k, v, S_new, shift_new, conv_new


def setup_inputs(seed: int = 0) -> dict:
    key = jax.random.key(seed)
    ks = jax.random.split(key, 40)
    nrm = lambda i, shape, scale: jax.random.normal(ks[i], shape, jnp.float32) * scale
    n_pages = PAST_LEN // PAGE_SIZE
    used = DEC_BATCH * n_pages
    n_pool = used + max(1, used // 4)
    page_table = jax.random.permutation(ks[0], n_pool)[:used].reshape(DEC_BATCH, n_pages).astype(jnp.int32)
    return {
        'x_prompt': nrm(1, (BATCH, SEQ, D_MODEL), 1.0),
        'x_sample': nrm(2, (DEC_BATCH, DEC_SEQ, D_MODEL), 1.0),
        'state_wkv': nrm(3, (DEPTH, DEC_BATCH, H_A, HD_A, HD_A), 0.5),
        'state_shift': nrm(4, (DEPTH, DEC_BATCH, C_SHIFT), 1.0),
        'state_conv': nrm(5, (DEPTH, DEC_BATCH, CONV_W - 1, D_FF), 1.0),
        'cache_k': nrm(6, (DEPTH, n_pool, PAGE_SIZE, H_B, HD_B), 1.0),
        'cache_v': nrm(7, (DEPTH, n_pool, PAGE_SIZE, H_B, HD_B), 1.0),
        'page_table': page_table,
        'rel_bias': nrm(8, (H_B, N_BUCKETS), 0.5),
        'ln_attn_pre': 1.0 + nrm(9, (DEPTH, D_MODEL), 0.05),
        'ln_attn_post': 1.0 + nrm(10, (DEPTH, D_MODEL), 0.05),
        'ln_ffn_pre': 1.0 + nrm(11, (DEPTH, D_MODEL), 0.05),
        'ln_ffn_post': 1.0 + nrm(12, (DEPTH, D_MODEL), 0.05),
        'w_in': nrm(13, (DEPTH, D_MODEL, C_IN), D_MODEL ** -0.5),
        'mu_shift': jax.random.uniform(ks[14], (DEPTH, C_SHIFT), jnp.float32),
        'w_decay_up': nrm(15, (DEPTH, LORA_W, C_A), LORA_W ** -0.5),
        'w_a_up': nrm(16, (DEPTH, LORA_A, C_A), LORA_A ** -0.5),
        'w_g_up': nrm(17, (DEPTH, LORA_G, C_A), LORA_G ** -0.5),
        'decay_base': jax.random.uniform(ks[18], (DEPTH, C_A), jnp.float32, minval=-3.0, maxval=1.0),
        'a_base': nrm(19, (DEPTH, C_A), 0.1),
        'k_k': 0.85 + nrm(20, (DEPTH, C_A), 0.05),
        'k_a': 1.0 + nrm(21, (DEPTH, C_A), 0.05),
        'r_k': nrm(22, (DEPTH, H_A, HD_A), 0.1),
        'gn_w': 1.0 + nrm(23, (DEPTH, C_A), 0.05),
        'gn_b': nrm(24, (DEPTH, C_A), 0.02),
        'w_branch_a': nrm(25, (DEPTH, C_A, D_MODEL), C_A ** -0.5),
        'w_branch_b': nrm(26, (DEPTH, C_B, D_MODEL), C_B ** -0.5),
        'w_out': nrm(27, (DEPTH, D_MODEL, D_MODEL), D_MODEL ** -0.5),
        'w_ffn_up': nrm(28, (DEPTH, D_MODEL, 2 * D_FF), D_MODEL ** -0.5),
        'ffn_conv_w': nrm(29, (DEPTH, CONV_W, D_FF), CONV_W ** -0.5),
        'ffn_conv_b': nrm(30, (DEPTH, D_FF), 0.02),
        'w_ffn_down': nrm(31, (DEPTH, D_FF, D_MODEL), D_FF ** -0.5),
    }


def reference(x_prompt, x_sample, state_wkv, state_shift, state_conv, cache_k, cache_v, page_table, rel_bias,
              ln_attn_pre, ln_attn_post, ln_ffn_pre, ln_ffn_post, w_in, mu_shift, w_decay_up, w_a_up, w_g_up,
              decay_base, a_base, k_k, k_a, r_k, gn_w, gn_b, w_branch_a, w_branch_b, w_out, w_ffn_up, ffn_conv_w,
              ffn_conv_b, w_ffn_down):
    bp, tp = x_prompt.shape[0], x_prompt.shape[1]
    bs, ts = x_sample.shape[0], x_sample.shape[1]
    past_len = page_table.shape[1] * cache_k.shape[2]
    pos_p = jnp.arange(tp, dtype=jnp.int32)
    pos_s = past_len + jnp.arange(ts, dtype=jnp.int32)
    dt = x_prompt.dtype
    xp, xs = x_prompt, x_sample
    kp, vp, wp, shp, cvp = [], [], [], [], []
    ksm, vsm, wsm, shs, cvs = [], [], [], [], []
    for l in range(DEPTH):
        lp = dict(ln_attn_pre=ln_attn_pre[l], ln_attn_post=ln_attn_post[l], ln_ffn_pre=ln_ffn_pre[l],
                  ln_ffn_post=ln_ffn_post[l], w_in=w_in[l], mu_shift=mu_shift[l], w_decay_up=w_decay_up[l],
                  w_a_up=w_a_up[l], w_g_up=w_g_up[l], decay_base=decay_base[l], a_base=a_base[l], k_k=k_k[l],
                  k_a=k_a[l], r_k=r_k[l], gn_w=gn_w[l], gn_b=gn_b[l], w_branch_a=w_branch_a[l],
                  w_branch_b=w_branch_b[l], w_out=w_out[l], w_ffn_up=w_ffn_up[l], ffn_conv_w=ffn_conv_w[l],
                  ffn_conv_b=ffn_conv_b[l], w_ffn_down=w_ffn_down[l])
        xp, k1, v1, s1, sh1, c1 = layer_forward(
            xp, pos_p, jnp.zeros((bp, H_A, HD_A, HD_A), dt), jnp.zeros((bp, C_SHIFT), dt),
            jnp.zeros((bp, CONV_W - 1, D_FF), dt), None, None, rel_bias, lp)
        pk = cache_k[l][page_table].reshape(bs, past_len, H_B, HD_B)
        pv = cache_v[l][page_table].reshape(bs, past_len, H_B, HD_B)
        xs, k2, v2, s2, sh2, c2 = layer_forward(
            xs, pos_s, state_wkv[l], state_shift[l], state_conv[l], pk, pv, rel_bias, lp)
        kp.append(k1); vp.append(v1); wp.append(s1); shp.append(sh1); cvp.append(c1)
        ksm.append(k2); vsm.append(v2); wsm.append(s2); shs.append(sh2); cvs.append(c2)
    return (xp, xs, jnp.stack(kp), jnp.stack(vp), jnp.stack(wp), jnp.stack(shp), jnp.stack(cvp),
            jnp.stack(ksm), jnp.stack(vsm), jnp.stack(wsm), jnp.stack(shs), jnp.stack(cvs))
```

```python
import functools
import math

import jax
import jax.numpy as jnp
from jax import lax
from jax.experimental import pallas as pl
from jax.experimental.pallas import tpu as pltpu

F32 = jnp.float32
BF16 = jnp.bfloat16
HI = lax.Precision.HIGHEST

D_MODEL = 2048
HD_A = 64
C_A = D_MODEL // 2
H_A = C_A // HD_A
LORA_W = C_A // 16
LORA_A = C_A // 16
LORA_G = C_A // 8
C_SHIFT = 3 * C_A + LORA_W + LORA_A + LORA_G
GN_EPS = 64e-5
HD_B = 128
C_B = D_MODEL // 2
H_B = C_B // HD_B
MOBA_BLOCK = 256
MOBA_TOPK = 3
N_BUCKETS = 32
C_IN = C_SHIFT + 3 * C_B + 2 * D_MODEL
D_FF = 5632
CONV_W = 3
EPS = 1e-6

COL_Q = C_SHIFT
COL_K = C_SHIFT + C_B
COL_V = C_SHIFT + 2 * C_B
COL_GA = C_SHIFT + 3 * C_B
COL_GB = COL_GA + D_MODEL

NEG = -1e30
T5_BUCKET_START = (0, 1, 2, 3, 4, 5, 6, 7, 8, 9, 10, 11, 12, 13, 14, 15, 16, 19, 21, 24, 27, 31,
                   35, 40, 46, 52, 59, 67, 77, 87, 99, 113)

VMEM_LIMIT = 56 * 1024 * 1024
RWKV_HEADS_PER_STEP = 2
RWKV_CHUNK = 64


def _params(*sem):
    return pltpu.CompilerParams(dimension_semantics=sem, vmem_limit_bytes=VMEM_LIMIT)


def _sigmoid(x):
    return 1.0 / (1.0 + jnp.exp(-x))


def _rms(x, g):
    return x * lax.rsqrt(jnp.mean(x * x, axis=-1, keepdims=True) + EPS) * g


def _rmsnorm_body(x_ref, g_ref, o_ref):
    o_ref[...] = _rms(x_ref[...], g_ref[...]).astype(o_ref.dtype)


def rmsnorm(x, g, tm):
    m, d = x.shape
    return pl.pallas_call(
        _rmsnorm_body,
        grid=(m // tm,),
        in_specs=[pl.BlockSpec((tm, d), lambda i: (i, 0)), pl.BlockSpec((1, d), lambda i: (0, 0))],
        out_specs=pl.BlockSpec((tm, d), lambda i: (i, 0)),
        out_shape=jax.ShapeDtypeStruct((m, d), BF16),
        compiler_params=_params("parallel"),
        name="rmsnorm",
    )(x, g.reshape(1, d))


def _resid_norm_body(x_ref, y_ref, gp_ref, gn_ref, xo_ref, ho_ref):
    xn = x_ref[...] + _rms(y_ref[...], gp_ref[...])
    xo_ref[...] = xn
    ho_ref[...] = _rms(xn, gn_ref[...]).astype(ho_ref.dtype)


def _resid_body(x_ref, y_ref, gp_ref, xo_ref):
    xo_ref[...] = x_ref[...] + _rms(y_ref[...], gp_ref[...])


def resid_norm(x, y, g_post, g_next, tm):
    m, d = x.shape
    row = pl.BlockSpec((tm, d), lambda i: (i, 0))
    vec = pl.BlockSpec((1, d), lambda i: (0, 0))
    if g_next is None:
        return pl.pallas_call(
            _resid_body, grid=(m // tm,), in_specs=[row, row, vec], out_specs=row,
            out_shape=jax.ShapeDtypeStruct((m, d), F32), compiler_params=_params("parallel"),
            name="resid",
        )(x, y, g_post.reshape(1, d)), None
    return pl.pallas_call(
        _resid_norm_body, grid=(m // tm,), in_specs=[row, row, vec, vec], out_specs=[row, row],
        out_shape=[jax.ShapeDtypeStruct((m, d), F32), jax.ShapeDtypeStruct((m, d), BF16)],
        compiler_params=_params("parallel"), name="resid_norm",
    )(x, y, g_post.reshape(1, d), g_next.reshape(1, d))


def _mm_body(x_ref, w_ref, o_ref):
    o_ref[...] = jnp.dot(x_ref[...], w_ref[...], preferred_element_type=F32).astype(o_ref.dtype)


def matmul(x, w, tm, tn, out_dtype=F32):
    m, k = x.shape
    n = w.shape[1]
    return pl.pallas_call(
        _mm_body,
        grid=(m // tm, n // tn),
        in_specs=[pl.BlockSpec((tm, k), lambda i, j: (i, 0)), pl.BlockSpec((k, tn), lambda i, j: (0, j))],
        out_specs=pl.BlockSpec((tm, tn), lambda i, j: (i, j)),
        out_shape=jax.ShapeDtypeStruct((m, n), out_dtype),
        compiler_params=_params("parallel", "parallel"),
        name="matmul",
    )(x, w)


def _merge_body(ya_ref, yb_ref, wa_ref, wb_ref, ga_ref, gb_ref, o_ref):
    a = jnp.dot(ya_ref[...], wa_ref[...], preferred_element_type=F32)
    b = jnp.dot(yb_ref[...], wb_ref[...], preferred_element_type=F32)
    o_ref[...] = (_sigmoid(ga_ref[...]) * a + _sigmoid(gb_ref[...]) * b).astype(o_ref.dtype)


def gated_merge(ya, yb, wa, wb, proj, tm, tn):
    m, ka = ya.shape
    kb = yb.shape[1]
    n = wa.shape[1]
    ga0, gb0 = COL_GA // tn, COL_GB // tn
    return pl.pallas_call(
        _merge_body,
        grid=(m // tm, n // tn),
        in_specs=[pl.BlockSpec((tm, ka), lambda i, j: (i, 0)),
                  pl.BlockSpec((tm, kb), lambda i, j: (i, 0)),
                  pl.BlockSpec((ka, tn), lambda i, j: (0, j)),
                  pl.BlockSpec((kb, tn), lambda i, j: (0, j)),
                  pl.BlockSpec((tm, tn), lambda i, j: (i, ga0 + j)),
                  pl.BlockSpec((tm, tn), lambda i, j: (i, gb0 + j))],
        out_specs=pl.BlockSpec((tm, tn), lambda i, j: (i, j)),
        out_shape=jax.ShapeDtypeStruct((m, n), BF16),
        compiler_params=_params("parallel", "parallel"),
        name="gated_merge",
    )(ya, yb, wa, wb, proj, proj)


def _shift_rows(x, prev_rows, n):
    rolled = pltpu.roll(x, n, 0)
    row = lax.broadcasted_iota(jnp.int32, x.shape, 0)
    p = prev_rows.shape[0]
    for r in range(n):
        rolled = jnp.where(row == r, prev_rows[p - n + r:p - n + r + 1, :], rolled)
    return rolled


def _rwkv_prep_body(*refs, has_prev_tile, t_valid, tt):
    if has_prev_tile:
        u_ref, pt_ref, st_ref = refs[:3]
        rest = refs[3:]
    else:
        u_ref, st_ref = refs[:2]
        pt_ref = None
        rest = refs[2:]
    (mu_ref, wdu_ref, wau_ref, wgu_ref, db_ref, ab_ref, kk_ref, ka_ref,
     r_o, lw_o, k_o, v_o, kk_o, a_o, g_o) = rest
    u = u_ref[0]
    prev = st_ref[0]
    if has_prev_tile:
        prev = jnp.where(pl.program_id(1) == 0, prev, pt_ref[0, 7:8, :])
    xs = u + (_shift_rows(u, prev, 1) - u) * mu_ref[...]
    o1, o2, o3 = C_A, 2 * C_A, 3 * C_A
    o4 = o3 + LORA_W
    o5 = o4 + LORA_A
    r, k, v = xs[:, :o1], xs[:, o1:o2], xs[:, o2:o3]
    wd, ad, gd = xs[:, o3:o4], xs[:, o4:o5], xs[:, o5:]
    dec = jnp.dot(jnp.tanh(wd).astype(BF16), wdu_ref[...], preferred_element_type=F32)
    z = -(db_ref[...] + dec)
    softplus = jnp.maximum(z, 0.0) + jnp.log1p(jnp.exp(-jnp.abs(z)))
    w_log = -softplus - 0.5
    a = _sigmoid(ab_ref[...] + jnp.dot(ad.astype(BF16), wau_ref[...], preferred_element_type=F32))
    g = jnp.dot(_sigmoid(gd).astype(BF16), wgu_ref[...], preferred_element_type=F32)
    lw = -jnp.exp(w_log)
    k_mod = k * (1.0 + (a - 1.0) * ka_ref[...])
    kk = k * kk_ref[...]
    if t_valid is not None:
        row = pl.program_id(1) * tt + lax.broadcasted_iota(jnp.int32, (tt, C_A), 0)
        keep = row < t_valid
        r, lw, k_mod, v, kk, a = (jnp.where(keep, z, 0.0) for z in (r, lw, k_mod, v, kk, a))
    r_o[0] = r
    lw_o[0] = lw
    k_o[0] = k_mod
    v_o[0] = v
    kk_o[0] = kk
    a_o[0] = a
    g_o[0] = g


def rwkv_prep(proj3, shift0, lp, tt, t_valid):
    b, t, _ = proj3.shape
    has_prev_tile = t > tt
    t_valid = None if t_valid == t else t_valid
    u_spec = pl.BlockSpec((1, tt, C_SHIFT), lambda bi, i: (bi, i, 0))
    in_specs = [u_spec]
    args = [proj3]
    if has_prev_tile:
        in_specs.append(pl.BlockSpec((1, 8, C_SHIFT), lambda bi, i: (bi, jnp.maximum(i * (tt // 8) - 1, 0), 0)))
        args.append(proj3)
    in_specs.append(pl.BlockSpec((1, 1, C_SHIFT), lambda bi, i: (bi, 0, 0)))
    args.append(shift0.reshape(b, 1, C_SHIFT))

    def full(x):
        return pl.BlockSpec(x.shape, lambda bi, i: (0,) * x.ndim)

    consts = [lp['mu_shift'].reshape(1, C_SHIFT), lp['w_decay_up_bf'], lp['w_a_up_bf'], lp['w_g_up_bf'],
              lp['decay_base'].reshape(1, C_A), lp['a_base'].reshape(1, C_A), lp['k_k'].reshape(1, C_A),
              lp['k_a'].reshape(1, C_A)]
    in_specs += [full(c) for c in consts]
    args += consts
    o_spec = pl.BlockSpec((1, tt, C_A), lambda bi, i: (bi, i, 0))
    o_shape = jax.ShapeDtypeStruct((b, t, C_A), F32)
    return pl.pallas_call(
        functools.partial(_rwkv_prep_body, has_prev_tile=has_prev_tile, t_valid=t_valid, tt=tt),
        grid=(b, t // tt), in_specs=in_specs, out_specs=[o_spec] * 7, out_shape=[o_shape] * 7,
        compiler_params=_params("parallel", "parallel"), name="rwkv_prep",
    )(*args)


def _dot_hi(a, b):
    return jnp.dot(a, b, precision=HI, preferred_element_type=F32)


def _dot_nt_hi(a, b):
    return lax.dot_general(a, b, (((1,), (1,)), ((), ())), precision=HI, preferred_element_type=F32)


def _dot_tn_hi(a, b):
    return lax.dot_general(a, b, (((0,), (0,)), ((), ())), precision=HI, preferred_element_type=F32)


def _rwkv_scan_body(r_ref, lw_ref, k_ref, v_ref, kk_ref, a_ref, g_ref, s0_ref, rk_ref, gw_ref, gb_ref,
                    y_ref, sT_ref, state, *, chunk, heads):
    ci = pl.program_id(2)

    @pl.when(ci == 0)
    def _():
        state[...] = s0_ref[0]

    row = lax.broadcasted_iota(jnp.int32, (chunk, chunk), 0)
    col = lax.broadcasted_iota(jnp.int32, (chunk, chunk), 1)
    lower_incl = row >= col
    lower_strict = row > col
    tri = jnp.where(lower_incl, 1.0, 0.0)
    eye = jnp.where(row == col, 1.0, 0.0)
    for h in range(heads):
        sl = slice(h * HD_A, (h + 1) * HD_A)
        r, lw, k, v = r_ref[0, :, sl], lw_ref[0, :, sl], k_ref[0, :, sl], v_ref[0, :, sl]
        kk, a, g = kk_ref[0, :, sl], a_ref[0, :, sl], g_ref[0, :, sl]
        kk = kk * lax.rsqrt(jnp.maximum(jnp.sum(kk * kk, axis=-1, keepdims=True), 1e-24))
        cl = _dot_hi(tri, lw)
        p_in, p_ex, p_inv = jnp.exp(cl), jnp.exp(cl - lw), jnp.exp(-cl)
        p_end = p_in[chunk - 1:chunk, :]
        alpha_bar = -kk * p_ex
        r_bar = r * p_in
        beta_t = kk * a * p_inv
        k_t = k * p_inv
        a_ab = jnp.where(lower_strict, _dot_nt_hi(alpha_bar, beta_t), 0.0)
        a_ak = jnp.where(lower_strict, _dot_nt_hi(alpha_bar, k_t), 0.0)
        a_rb = jnp.where(lower_incl, _dot_nt_hi(r_bar, beta_t), 0.0)
        a_rk = jnp.where(lower_incl, _dot_nt_hi(r_bar, k_t), 0.0)
        inv = eye + a_ab
        npow = a_ab
        span = 2
        while span < chunk:
            npow = _dot_hi(npow, npow)
            inv = inv + _dot_hi(npow, inv)
            span *= 2
        s_t = state[h]
        u = _dot_hi(inv, _dot_hi(alpha_bar, s_t) + _dot_hi(a_ak, v))
        o = _dot_hi(r_bar, s_t) + _dot_hi(a_rb, u) + _dot_hi(a_rk, v)
        decay_diag = jnp.where(lax.broadcasted_iota(jnp.int32, (HD_A, HD_A), 0)
                               == lax.broadcasted_iota(jnp.int32, (HD_A, HD_A), 1), p_end, 0.0)
        state[h] = _dot_hi(decay_diag, s_t) + _dot_tn_hi(beta_t * p_end, u) + _dot_tn_hi(k_t * p_end, v)
        mean = jnp.mean(o, axis=-1, keepdims=True)
        var = jnp.mean((o - mean) ** 2, axis=-1, keepdims=True)
        on = (o - mean) * lax.rsqrt(var + GN_EPS) * gw_ref[:, sl] + gb_ref[:, sl]
        bonus = jnp.sum(r * k * rk_ref[:, sl], axis=-1, keepdims=True) * v
        y_ref[0, :, sl] = ((on + bonus) * g).astype(y_ref.dtype)

    @pl.when(ci == pl.num_programs(2) - 1)
    def _():
        sT_ref[0] = state[...]


def rwkv_scan(prep, s0_t, lp, chunk):
    b, t, _ = prep[0].shape
    hg = RWKV_HEADS_PER_STEP
    wlanes = hg * HD_A
    seq = pl.BlockSpec((1, chunk, wlanes), lambda bi, hi, ci: (bi, ci, hi))
    st = pl.BlockSpec((1, hg, HD_A, HD_A), lambda bi, hi, ci: (bi, hi, 0, 0))
    vec = pl.BlockSpec((1, wlanes), lambda bi, hi, ci: (0, hi))
    y, s_t = pl.pallas_call(
        functools.partial(_rwkv_scan_body, chunk=chunk, heads=hg),
        grid=(b, H_A // hg, t // chunk),
        in_specs=[seq] * 7 + [st, vec, vec, vec],
        out_specs=[seq, st],
        out_shape=[jax.ShapeDtypeStruct((b, t, C_A), BF16), jax.ShapeDtypeStruct((b, H_A, HD_A, HD_A), F32)],
        scratch_shapes=[pltpu.VMEM((hg, HD_A, HD_A), F32)],
        compiler_params=_params("parallel", "parallel", "arbitrary"),
        name="rwkv_scan",
    )(*prep, s0_t, lp['r_k'].reshape(1, C_A), lp['gn_w'].reshape(1, C_A), lp['gn_b'].reshape(1, C_A))
    return y, s_t


def _t5_bias(rel, rb_ref, h):
    val = jnp.full(rel.shape, rb_ref[h, 0], F32)
    for bkt in range(1, N_BUCKETS):
        val = jnp.where(rel >= T5_BUCKET_START[bkt], rb_ref[h, bkt], val)
    return val


def _bias_tiles_body(rb_ref, o_ref, *, blk):
    h = pl.program_id(0)
    r = lax.broadcasted_iota(jnp.int32, (blk, blk), 0)
    c = lax.broadcasted_iota(jnp.int32, (blk, blk), 1)
    rel = r - c
    o_ref[0, 0] = jnp.where(rel >= 0, _t5_bias(rel, rb_ref, h), NEG)
    o_ref[0, 1] = _t5_bias(rel + blk, rb_ref, h)


def bias_tiles(rel_bias, blk):
    return pl.pallas_call(
        functools.partial(_bias_tiles_body, blk=blk),
        grid=(H_B,),
        in_specs=[pl.BlockSpec(memory_space=pltpu.SMEM)],
        out_specs=pl.BlockSpec((1, 2, blk, blk), lambda h: (h, 0, 0, 0)),
        out_shape=jax.ShapeDtypeStruct((H_B, 2, blk, blk), F32),
        compiler_params=_params("parallel"), name="bias_tiles",
    )(rel_bias)


def _moba_prompt_body(rb_ref, q_ref, k_ref, v_ref, bt_ref, o_ref, *, nb, blk):
    h = pl.program_id(1)
    i = pl.program_id(2)
    scale = HD_B ** -0.5
    q = q_ref[0]
    qb = q.astype(BF16)
    kmean = jnp.concatenate(
        [jnp.sum(k_ref[0, n * blk:(n + 1) * blk, :], axis=0, keepdims=True) for n in range(nb)], axis=0) * (1.0 / blk)
    gate = _dot_nt_hi(q, kmean)
    lane = lax.broadcasted_iota(jnp.int32, (blk, nb), 1)
    rank = jnp.zeros((blk, nb), F32)
    for m in range(nb):
        gm = gate[:, m:m + 1]
        beats = (gm > gate) | ((gm == gate) & (lane > m))
        rank = rank + jnp.where(beats, 1.0, 0.0) * jnp.where(m < i, 1.0, 0.0)
    sel = jnp.where((lane < i) & (rank < MOBA_TOPK), 1.0, 0.0)

    def scores(kj):
        return lax.dot_general(qb, kj.astype(BF16), (((1,), (1,)), ((), ())), preferred_element_type=F32) * scale

    own = pl.multiple_of(i * blk, blk)
    s = scores(k_ref[0, pl.ds(own, blk), :]) + bt_ref[0, 0]
    m0 = jnp.max(s, axis=-1, keepdims=True)
    p = jnp.exp(s - m0)
    l0 = jnp.sum(p, axis=-1, keepdims=True)
    acc0 = jnp.dot(p.astype(BF16), v_ref[0, pl.ds(own, blk), :].astype(BF16), preferred_element_type=F32)
    b_far = rb_ref[h, N_BUCKETS - 1]
    bias_prev = bt_ref[0, 1]

    def body(j, carry):
        m_run, l_run, acc = carry
        start = pl.multiple_of(j * blk, blk)
        is_prev = jnp.where(i - j == 1, 1.0, 0.0)
        bias = bias_prev * is_prev + b_far * (1.0 - is_prev)
        chosen = jnp.sum(jnp.where(lane == j, sel, 0.0), axis=1, keepdims=True)
        s = jnp.where(chosen > 0.5, scores(k_ref[0, pl.ds(start, blk), :]) + bias, NEG)
        m_new = jnp.maximum(m_run, jnp.max(s, axis=-1, keepdims=True))
        alpha = jnp.exp(m_run - m_new)
        p = jnp.exp(s - m_new)
        l_new = alpha * l_run + jnp.sum(p, axis=-1, keepdims=True)
        acc = alpha * acc + jnp.dot(p.astype(BF16), v_ref[0, pl.ds(start, blk), :].astype(BF16),
                                    preferred_element_type=F32)
        return m_new, l_new, acc

    _, l_fin, acc = lax.fori_loop(0, i, body, (m0, l0, acc0))
    o_ref[0] = (acc / l_fin).astype(o_ref.dtype)


def moba_prompt(proj3, rel_bias, tiles):
    b, t, _ = proj3.shape
    blk = MOBA_BLOCK
    nb = t // blk
    qc, kc, vc = COL_Q // HD_B, COL_K // HD_B, COL_V // HD_B
    return pl.pallas_call(
        functools.partial(_moba_prompt_body, nb=nb, blk=blk),
        grid=(b, H_B, nb),
        in_specs=[pl.BlockSpec(memory_space=pltpu.SMEM),
                  pl.BlockSpec((1, blk, HD_B), lambda bi, h, i: (bi, i, qc + h)),
                  pl.BlockSpec((1, t, HD_B), lambda bi, h, i: (bi, 0, kc + h)),
                  pl.BlockSpec((1, t, HD_B), lambda bi, h, i: (bi, 0, vc + h)),
                  pl.BlockSpec((1, 2, blk, blk), lambda bi, h, i: (h, 0, 0, 0))],
        out_specs=pl.BlockSpec((1, blk, HD_B), lambda bi, h, i: (bi, i, h)),
        out_shape=jax.ShapeDtypeStruct((b, t, C_B), BF16),
        compiler_params=_params("parallel", "parallel", "arbitrary"), name="moba_prompt",
    )(rel_bias, proj3, proj3, proj3, tiles)


def _page_mean_body(pt_ref, k0_ref, k1_ref, o_ref):
    del pt_ref
    o_ref[0, 0] = (jnp.sum(k0_ref[0], axis=0, keepdims=True) + jnp.sum(k1_ref[0], axis=0, keepdims=True)) * (
        1.0 / MOBA_BLOCK)


def page_block_means(cache_k2, pages, b, n_past_blocks, pages_per_seq):
    page_size = cache_k2.shape[1]
    width = cache_k2.shape[2]

    def page(which):
        return pl.BlockSpec((1, page_size, width), lambda bi, n, pt: (pt[bi * pages_per_seq + 2 * n + which], 0, 0))

    out = pl.pallas_call(
        _page_mean_body,
        grid_spec=pltpu.PrefetchScalarGridSpec(
            num_scalar_prefetch=1, grid=(b, n_past_blocks), in_specs=[page(0), page(1)],
            out_specs=pl.BlockSpec((1, 1, 1, width), lambda bi, n, pt: (bi, n, 0, 0))),
        out_shape=jax.ShapeDtypeStruct((b, n_past_blocks, 1, width), F32),
        compiler_params=_params("parallel", "parallel"), name="page_block_means",
    )(pages, cache_k2, cache_k2)
    return out.reshape(b, n_past_blocks, width)


def _gate_topk_body(q_ref, km_ref, o_ref, *, nq, n_blocks):
    gate = _dot_nt_hi(q_ref[0], km_ref[0])
    lane = lax.broadcasted_iota(jnp.int32, (nq, n_blocks), 1).astype(F32)
    out_lane = lax.broadcasted_iota(jnp.int32, (nq, 128), 1)
    out = jnp.zeros((nq, 128), F32)
    for r in range(MOBA_TOPK):
        best = jnp.max(gate, axis=1, keepdims=True)
        idx = jnp.min(jnp.where(gate == best, lane, float(n_blocks)), axis=1, keepdims=True)
        gate = jnp.where(lane == idx, -jnp.inf, gate)
        out = jnp.where(out_lane == r, idx, out)
    o_ref[0, 0] = out.astype(jnp.int32)


def gate_topk(proj3, kmean):
    b, nq, _ = proj3.shape
    n_blocks = kmean.shape[1]
    assert n_blocks >= MOBA_TOPK
    qc = COL_Q // HD_B
    out = pl.pallas_call(
        functools.partial(_gate_topk_body, nq=nq, n_blocks=n_blocks),
        grid=(b, H_B),
        in_specs=[pl.BlockSpec((1, nq, HD_B), lambda bi, h: (bi, 0, qc + h)),
                  pl.BlockSpec((1, n_blocks, HD_B), lambda bi, h: (bi, 0, h))],
        out_specs=pl.BlockSpec((1, 1, nq, 128), lambda bi, h: (bi, h, 0, 0)),
        out_shape=jax.ShapeDtypeStruct((b, H_B, nq, 128), jnp.int32),
        compiler_params=_params("parallel", "parallel"), name="gate_topk",
    )(proj3, kmean)
    return out[..., :MOBA_TOPK]


def _moba_sample_body(pt_ref, sel_ref, rb_ref, q_ref, kn_ref, vn_ref, k0_ref, k1_ref, v0_ref, v1_ref, o_ref,
                      m_sc, l_sc, acc_sc, *, nq, nq_rows, past_len, page_size):
    bi, h, qi, si = pl.program_id(0), pl.program_id(1), pl.program_id(2), pl.program_id(3)
    del pt_ref
    scale = HD_B ** -0.5
    q = q_ref[0, pl.ds(qi, 1), :]

    @pl.when(si == 0)
    def _():
        t_idx = lax.broadcasted_iota(jnp.int32, (nq_rows, 1), 0)
        rel = qi - t_idx
        s = jnp.sum(kn_ref[0] * q, axis=-1, keepdims=True) * scale + _t5_bias(rel, rb_ref, h)
        s = jnp.where(rel >= 0, s, NEG)
        m0 = jnp.max(s, axis=0, keepdims=True)
        p = jnp.exp(s - m0)
        m_sc[...] = jnp.broadcast_to(m0, m_sc.shape)
        l_sc[...] = jnp.broadcast_to(jnp.sum(p, axis=0, keepdims=True), l_sc.shape)
        acc_sc[...] = jnp.broadcast_to(jnp.sum(p * vn_ref[0], axis=0, keepdims=True), acc_sc.shape)

    blk_id = sel_ref[((bi * H_B + h) * nq + qi) * MOBA_TOPK + si]
    keys = jnp.concatenate([k0_ref[0], k1_ref[0]], axis=0).astype(BF16)
    vals = jnp.concatenate([v0_ref[0], v1_ref[0]], axis=0).astype(BF16)
    q8 = jnp.broadcast_to(q, (8, HD_B)).astype(BF16)
    c = lax.broadcasted_iota(jnp.int32, (8, 2 * page_size), 1)
    rel = past_len + qi - (blk_id * (2 * page_size) + c)
    s = lax.dot_general(q8, keys, (((1,), (1,)), ((), ())), preferred_element_type=F32) * scale
    s = s + _t5_bias(rel, rb_ref, h)
    m_run, l_run = m_sc[...], l_sc[...]
    m_new = jnp.maximum(m_run, jnp.max(s, axis=-1, keepdims=True))
    alpha = jnp.exp(m_run - m_new)
    p = jnp.exp(s - m_new)
    l_new = alpha * l_run + jnp.sum(p, axis=-1, keepdims=True)
    acc = alpha * acc_sc[...] + jnp.dot(p.astype(BF16), vals, preferred_element_type=F32)
    m_sc[...] = m_new
    l_sc[...] = l_new
    acc_sc[...] = acc

    @pl.when(si == MOBA_TOPK - 1)
    def _():
        o_ref[0, 0, 0] = (acc / l_new)[0:1, :]


def moba_sample(proj3, nq, cache_k2, cache_v2, pages, sel, rel_bias, past_len, pages_per_seq):
    b, nq_rows, _ = proj3.shape
    page_size = cache_k2.shape[1]
    assert MOBA_BLOCK == 2 * page_size and past_len % MOBA_BLOCK == 0 and nq <= MOBA_BLOCK
    qc, kc, vc = COL_Q // HD_B, COL_K // HD_B, COL_V // HD_B

    def new_tok(col):
        return pl.BlockSpec((1, nq_rows, HD_B), lambda bi, h, qi, si, pt, sl: (bi, 0, col + h))

    def page(which):
        def index(bi, h, qi, si, pt, sl):
            blk = sl[((bi * H_B + h) * nq + qi) * MOBA_TOPK + si]
            return (pt[bi * pages_per_seq + 2 * blk + which], 0, h)
        return pl.BlockSpec((1, page_size, HD_B), index)

    out = pl.pallas_call(
        functools.partial(_moba_sample_body, nq=nq, nq_rows=nq_rows, past_len=past_len, page_size=page_size),
        grid_spec=pltpu.PrefetchScalarGridSpec(
            num_scalar_prefetch=2, grid=(b, H_B, nq, MOBA_TOPK),
            in_specs=[pl.BlockSpec(memory_space=pltpu.SMEM), new_tok(qc), new_tok(kc), new_tok(vc),
                      page(0), page(1), page(0), page(1)],
            out_specs=pl.BlockSpec((1, 1, 1, 1, HD_B), lambda bi, h, qi, si, pt, sl: (bi, qi, h, 0, 0)),
            scratch_shapes=[pltpu.VMEM((8, 1), F32), pltpu.VMEM((8, 1), F32), pltpu.VMEM((8, HD_B), F32)]),
        out_shape=jax.ShapeDtypeStruct((b, nq, H_B, 1, HD_B), F32),
        compiler_params=_params("parallel", "parallel", "parallel", "arbitrary"), name="moba_sample",
    )(pages, sel.reshape(-1), rel_bias, proj3, proj3, proj3, cache_k2, cache_k2, cache_v2, cache_v2)
    return out.reshape(b * nq, C_B).astype(BF16)


def _conv_gate_body(*refs, has_prev_tile):
    if has_prev_tile:
        u_ref, gt_ref, pt_ref, buf_ref, cw_ref, cb_ref, o_ref = refs
    else:
        u_ref, gt_ref, buf_ref, cw_ref, cb_ref, o_ref = refs
        pt_ref = None
    u = u_ref[0]
    prev = buf_ref[0]
    if has_prev_tile:
        prev = jnp.where(pl.program_id(1) == 0, prev, pt_ref[0, 8 - (CONV_W - 1):, :])
    c = cb_ref[...] + cw_ref[CONV_W - 1:CONV_W, :] * u
    for j in range(CONV_W - 1):
        c = c + cw_ref[j:j + 1, :] * _shift_rows(u, prev, CONV_W - 1 - j)
    gelu = 0.5 * c * (1.0 + jnp.tanh(math.sqrt(2.0 / math.pi) * (c + 0.044715 * (c * c * c))))
    o_ref[0] = (gelu * gt_ref[0]).astype(o_ref.dtype)


def conv_gate(ug3, conv0, conv_w, conv_b, tt, tc):
    b, t, _ = ug3.shape
    has_prev_tile = t > tt
    n_c = D_FF // tc
    in_specs = [pl.BlockSpec((1, tt, tc), lambda bi, i, j: (bi, i, j)),
                pl.BlockSpec((1, tt, tc), lambda bi, i, j: (bi, i, n_c + j))]
    args = [ug3, ug3]
    if has_prev_tile:
        in_specs.append(pl.BlockSpec((1, 8, tc), lambda bi, i, j: (bi, jnp.maximum(i * (tt // 8) - 1, 0), j)))
        args.append(ug3)
    in_specs += [pl.BlockSpec((1, CONV_W - 1, tc), lambda bi, i, j: (bi, 0, j)),
                 pl.BlockSpec((CONV_W, tc), lambda bi, i, j: (0, j)),
                 pl.BlockSpec((1, tc), lambda bi, i, j: (0, j))]
    args += [conv0, conv_w, conv_b.reshape(1, D_FF)]
    return pl.pallas_call(
        functools.partial(_conv_gate_body, has_prev_tile=has_prev_tile),
        grid=(b, t // tt, n_c), in_specs=in_specs,
        out_specs=pl.BlockSpec((1, tt, tc), lambda bi, i, j: (bi, i, j)),
        out_shape=jax.ShapeDtypeStruct((b, t, D_FF), BF16),
        compiler_params=_params("parallel", "parallel", "parallel"), name="conv_gate",
    )(*args)


def _pad_rows(x, t_pad):
    return jnp.pad(x, ((0, 0), (0, t_pad - x.shape[1]), (0, 0)))


def layer_forward(x, h, s0, shift0, conv0, lp, g_next, attend, tm, tt):
    b, t, d = x.shape
    n = b * t
    assert t >= CONV_W - 1
    proj = matmul(h, lp['w_in_bf'], tm, 256)
    proj3 = proj.reshape(b, t, C_IN)
    t_pad = -(-t // tt) * tt
    proj3p = _pad_rows(proj3, t_pad)

    prep = rwkv_prep(proj3p, shift0, lp, tt, t)
    chunk = RWKV_CHUNK if t_pad % RWKV_CHUNK == 0 else 8
    y_a, s_t = rwkv_scan(prep, jnp.swapaxes(s0, -1, -2), lp, chunk)
    y_a = y_a[:, :t].reshape(n, C_A)
    s_new = jnp.swapaxes(s_t, -1, -2)

    y_b = attend(proj3p, t).reshape(n, C_B)

    mix = gated_merge(y_a, y_b, lp['w_branch_a_bf'], lp['w_branch_b_bf'], proj, tm, 256)
    attn = matmul(mix, lp['w_out_bf'], tm, 256)
    x2d, h2 = resid_norm(x.reshape(n, d), attn, lp['ln_attn_post'], lp['ln_ffn_pre'], min(tm, 256))

    ug = matmul(h2, lp['w_ffn_up_bf'], tm, 256)
    ug3 = ug.reshape(b, t, 2 * D_FF)
    act = conv_gate(_pad_rows(ug3, t_pad), conv0, lp['ffn_conv_w'], lp['ffn_conv_b'], tt, D_FF // 2)[:, :t]
    f = matmul(act.reshape(n, D_FF), lp['w_ffn_down_bf'], tm, 256)
    x_out, h_out = resid_norm(x2d, f, lp['ln_ffn_post'], g_next, min(tm, 256))

    k_new = proj3[:, :, COL_K:COL_V].reshape(b, t, H_B, HD_B)
    v_new = proj3[:, :, COL_V:COL_GA].reshape(b, t, H_B, HD_B)
    shift_new = proj3[:, t - 1, :C_SHIFT]
    conv_new = ug3[:, t - (CONV_W - 1):, :D_FF]
    return x_out.reshape(b, t, d), h_out, k_new, v_new, s_new, shift_new, conv_new


def kernel(x_prompt, x_sample, state_wkv, state_shift, state_conv, cache_k, cache_v, page_table, rel_bias,
           ln_attn_pre, ln_attn_post, ln_ffn_pre, ln_ffn_post, w_in, mu_shift, w_decay_up, w_a_up, w_g_up,
           decay_base, a_base, k_k, k_a, r_k, gn_w, gn_b, w_branch_a, w_branch_b, w_out, w_ffn_up, ffn_conv_w,
           ffn_conv_b, w_ffn_down):
    bp, tp, d = x_prompt.shape
    bs, ts, _ = x_sample.shape
    depth = w_in.shape[0]
    n_pool, page_size = cache_k.shape[1], cache_k.shape[2]
    pages_per_seq = page_table.shape[1]
    past_len = pages_per_seq * page_size
    n_past_blocks = past_len // MOBA_BLOCK
    cache_k2 = cache_k.reshape(depth * n_pool, page_size, C_B)
    cache_v2 = cache_v.reshape(depth * n_pool, page_size, C_B)
    tiles = bias_tiles(rel_bias, MOBA_BLOCK)

    tm_p, tt_p = 1024, 256
    tm_s, tt_s = bs * ts, 8
    xp, xs = x_prompt, x_sample
    hp = rmsnorm(xp.reshape(bp * tp, d), ln_attn_pre[0], 256)
    hs = rmsnorm(xs.reshape(bs * ts, d), ln_attn_pre[0], tm_s)
    outs_p, outs_s = [], []
    for l in range(depth):
        lp = dict(ln_attn_post=ln_attn_post[l], ln_ffn_pre=ln_ffn_pre[l], ln_ffn_post=ln_ffn_post[l],
                  mu_shift=mu_shift[l], decay_base=decay_base[l], a_base=a_base[l], k_k=k_k[l], k_a=k_a[l],
                  r_k=r_k[l], gn_w=gn_w[l], gn_b=gn_b[l], ffn_conv_w=ffn_conv_w[l], ffn_conv_b=ffn_conv_b[l],
                  w_in_bf=w_in[l].astype(BF16), w_decay_up_bf=w_decay_up[l].astype(BF16),
                  w_a_up_bf=w_a_up[l].astype(BF16), w_g_up_bf=w_g_up[l].astype(BF16),
                  w_branch_a_bf=w_branch_a[l].astype(BF16), w_branch_b_bf=w_branch_b[l].astype(BF16),
                  w_out_bf=w_out[l].astype(BF16), w_ffn_up_bf=w_ffn_up[l].astype(BF16),
                  w_ffn_down_bf=w_ffn_down[l].astype(BF16))
        g_next = ln_attn_pre[l + 1] if l + 1 < depth else None

        xp, hp, k1, v1, s1, sh1, c1 = layer_forward(
            xp, hp, jnp.zeros((bp, H_A, HD_A, HD_A), F32), jnp.zeros((bp, C_SHIFT), F32),
            jnp.zeros((bp, CONV_W - 1, D_FF), F32), lp, g_next,
            lambda proj3, t_valid: moba_prompt(proj3, rel_bias, tiles), tm_p, tt_p)

        pages = (page_table + l * n_pool).reshape(-1).astype(jnp.int32)

        def attend_sample(proj3, t_valid):
            kmean = page_block_means(cache_k2, pages, bs, n_past_blocks, pages_per_seq)
            sel = gate_topk(proj3, kmean)[:, :, :t_valid]
            return moba_sample(proj3, t_valid, cache_k2, cache_v2, pages, sel, rel_bias, past_len, pages_per_seq)

        xs, hs, k2, v2, s2, sh2, c2 = layer_forward(
            xs, hs, state_wkv[l], state_shift[l], state_conv[l], lp, g_next, attend_sample, tm_s, tt_s)
        outs_p.append((k1, v1, s1, sh1, c1))
        outs_s.append((k2, v2, s2, sh2, c2))

    stack = lambda outs, i: jnp.stack([o[i] for o in outs])
    return (xp, xs) + tuple(stack(outs_p, i) for i in range(5)) + tuple(stack(outs_s, i) for i in range(5))
```
